```python
import math
import jax, jax.numpy as jnp
from jax import lax
import numpy as np

D_MODEL = 2048
BATCH = 4
SEQ = 2048
DEPTH = 4
DEC_BATCH = 128
DEC_SEQ = 4
PAST_LEN = 16384
PAGE_SIZE = 128

N_MEM = 256
EPS = 1e-6
D_FF = 5632
GDN_DK = 128
GDN_DV = 128
GDN_HK = D_MODEL // GDN_DK
GDN_HV = 2 * GDN_HK
GDN_KD = GDN_HK * GDN_DK
GDN_VD = GDN_HV * GDN_DV
GDN_CONV_DIM = 2 * GDN_KD + GDN_VD
GDN_CONV_W = 4
GDN_CHUNK = 64
GDN_IN_DIM = GDN_CONV_DIM + GDN_VD + 2 * GDN_HV
S5_GROUP = 16
S5_GROUPS = D_MODEL // S5_GROUP
S5_P = 64
XA_HEADS = 4
XA_HD = D_MODEL // XA_HEADS
N_A = (DEPTH + 1) // 2
N_B = DEPTH // 2

kernel_name = 'hybrid_gdn_s5_macaron_xattn_step'


def rmsnorm(x, g):
    xf = x.astype(jnp.float32)
    y = xf * lax.rsqrt(jnp.mean(xf * xf, axis=-1, keepdims=True) + EPS)
    return (y * g.astype(jnp.float32)).astype(x.dtype)


def swiglu(h, w_up, w_down):
    gate, up = jnp.split(h @ w_up, 2, axis=-1)
    return (jax.nn.silu(gate) * up) @ w_down


def l2norm(t):
    return t * lax.rsqrt(jnp.sum(t * t, axis=-1, keepdims=True) + EPS)


def to_blocks(t, n, c):
    bt, l, h = t.shape[:3]
    t = t.reshape((bt, n, c, h) + t.shape[3:])
    return jnp.moveaxis(t, (1, 3), (0, 2))


def gated_delta_rule(q, k, v, g, beta, s0):
    bt, l, h, dk = q.shape
    dv = v.shape[-1]
    c = GDN_CHUNK if l % GDN_CHUNK == 0 else l
    n = l // c
    qb, kb, vb = to_blocks(q, n, c), to_blocks(k, n, c), to_blocks(v, n, c)
    gc = jnp.cumsum(to_blocks(g, n, c), axis=-1)
    bb = to_blocks(beta, n, c)
    idx = jnp.arange(c)
    incl = idx[:, None] >= idx[None, :]
    strict = idx[:, None] > idx[None, :]
    decay = jnp.exp(jnp.where(incl, gc[..., :, None] - gc[..., None, :], -jnp.inf))
    kbeta = kb * bb[..., None]
    vbeta = vb * bb[..., None]
    lmat = jnp.where(strict, jnp.einsum('nbhid,nbhjd->nbhij', kbeta, kb) * decay, 0.0)
    eye = jnp.eye(c, dtype=jnp.float32)
    tinv = lax.linalg.triangular_solve(eye + lmat, jnp.broadcast_to(eye, lmat.shape),
                                       left_side=True, lower=True, unit_diagonal=True)
    u = jnp.einsum('nbhij,nbhje->nbhie', tinv, vbeta)
    w = jnp.einsum('nbhij,nbhjd->nbhid', tinv, kbeta * jnp.exp(gc)[..., None])
    attn = jnp.einsum('nbhid,nbhjd->nbhij', qb, kb) * decay
    qdec = qb * jnp.exp(gc)[..., None]
    kdec = kb * jnp.exp(gc[..., -1:] - gc)[..., None]
    glast = jnp.exp(gc[..., -1])

    def step(s, xs):
        u_i, w_i, a_i, qd_i, kd_i, gl_i = xs
        v_new = u_i - jnp.einsum('bhcd,bhde->bhce', w_i, s)
        o_i = jnp.einsum('bhcd,bhde->bhce', qd_i, s) + jnp.einsum('bhij,bhje->bhie', a_i, v_new)
        s = s * gl_i[..., None, None] + jnp.einsum('bhcd,bhce->bhde', kd_i, v_new)
        return s, o_i

    s_fin, o = lax.scan(step, s0, (u, w, attn, qdec, kdec, glast))
    o = jnp.moveaxis(o, (0, 2), (1, 3)).reshape(bt, l, h, dv)
    return o, s_fin


def gdn_mixer(h, s0, conv0, w_in, conv_w, a_log, dt_bias, norm_w, w_out):
    f32 = jnp.float32
    bt, l, _ = h.shape
    proj = (h @ w_in).astype(f32)
    o1 = GDN_CONV_DIM
    o2 = o1 + GDN_VD
    o3 = o2 + GDN_HV
    qkv, z, b, a = proj[..., :o1], proj[..., o1:o2], proj[..., o2:o3], proj[..., o3:]
    xp = jnp.concatenate([conv0.astype(f32), qkv], axis=1)
    cw = conv_w.astype(f32)
    c = xp[:, 0:l] * cw[:, 0]
    for j in range(1, GDN_CONV_W):
        c = c + xp[:, j:j + l] * cw[:, j]
    c = jax.nn.silu(c)
    q = l2norm(c[..., :GDN_KD].reshape(bt, l, GDN_HK, GDN_DK)) * (GDN_DK ** -0.5)
    k = l2norm(c[..., GDN_KD:2 * GDN_KD].reshape(bt, l, GDN_HK, GDN_DK))
    v = c[..., 2 * GDN_KD:].reshape(bt, l, GDN_HV, GDN_DV)
    rep = GDN_HV // GDN_HK
    q = jnp.repeat(q, rep, axis=2)
    k = jnp.repeat(k, rep, axis=2)
    beta = jax.nn.sigmoid(b)
    g = -jnp.exp(a_log.astype(f32)) * jax.nn.softplus(a + dt_bias.astype(f32))
    o, s_fin = gated_delta_rule(q, k, v, g, beta, s0.astype(f32))
    o = (o * lax.rsqrt(jnp.mean(o * o, axis=-1, keepdims=True) + EPS) * norm_w.astype(f32)
         * jax.nn.silu(z.reshape(bt, l, GDN_HV, GDN_DV)))
    out = o.reshape(bt, l, GDN_VD) @ w_out.astype(f32)
    return out, s_fin, xp[:, l:]


def s5_combine(e1, e2):
    a1r, a1i, b1r, b1i = e1
    a2r, a2i, b2r, b2i = e2
    return (a1r * a2r - a1i * a2i,
            a1r * a2i + a1i * a2r,
            a2r * b1r - a2i * b1i + b2r,
            a2r * b1i + a2i * b1r + b2i)


def s5_mixer(u, h0_re, h0_im, a_re, a_im, log_dt, b_re, b_im, c_re, c_im, d_skip, w_glu):
    f32 = jnp.float32
    bt, l, _ = u.shape
    uf = u.astype(f32)
    ug = uf.reshape(bt, l, S5_GROUPS, S5_GROUP)
    ar, ai = a_re.astype(f32), a_im.astype(f32)
    dt = jnp.exp(log_dt.astype(f32))[:, None]
    mag = jnp.exp(ar * dt)
    lb_re = mag * jnp.cos(ai * dt)
    lb_im = mag * jnp.sin(ai * dt)
    den = ar * ar + ai * ai
    nr = lb_re - 1.0
    f_re = (nr * ar + lb_im * ai) / den
    f_im = (lb_im * ar - nr * ai) / den
    br, bi = b_re.astype(f32), b_im.astype(f32)
    bb_re = f_re[..., None] * br - f_im[..., None] * bi
    bb_im = f_re[..., None] * bi + f_im[..., None] * br
    bu_re = jnp.einsum('blgc,gpc->lbgp', ug, bb_re)
    bu_im = jnp.einsum('blgc,gpc->lbgp', ug, bb_im)
    h0r, h0i = h0_re.astype(f32), h0_im.astype(f32)
    bu_re = bu_re.at[0].add(lb_re * h0r - lb_im * h0i)
    bu_im = bu_im.at[0].add(lb_re * h0i + lb_im * h0r)
    a_r = jnp.broadcast_to(lb_re, (l, 1) + lb_re.shape)
    a_i = jnp.broadcast_to(lb_im, (l, 1) + lb_im.shape)
    _, _, hs_re, hs_im = lax.associative_scan(s5_combine, (a_r, a_i, bu_re, bu_im), axis=0)
    y = (jnp.einsum('lbgp,gcp->blgc', hs_re, c_re.astype(f32))
         - jnp.einsum('lbgp,gcp->blgc', hs_im, c_im.astype(f32)))
    y = y.reshape(bt, l, D_MODEL) + d_skip.astype(f32) * uf
    y = jax.nn.gelu(y)
    ga, gb = jnp.split(y @ w_glu.astype(f32), 2, axis=-1)
    return ga * jax.nn.sigmoid(gb), hs_re[-1], hs_im[-1]


def cross_attn(h, mem_k, mem_v, w_q, w_o):
    f32 = jnp.float32
    bt, l, _ = h.shape
    q = (h @ w_q).astype(f32).reshape(bt, l, XA_HEADS, XA_HD)
    s = jnp.einsum('blhd,bmhd->bhlm', q, mem_k.astype(f32)) * (XA_HD ** -0.5)
    pr = jax.nn.softmax(s, axis=-1)
    o = jnp.einsum('bhlm,bmhd->blhd', pr, mem_v.astype(f32)).reshape(bt, l, D_MODEL)
    return o @ w_o.astype(f32)


def trunk(x, mem_k, mem_v, gdn_s, gdn_conv, s5_re, s5_im, p):
    new_s, new_conv, new_re, new_im = [], [], [], []
    for i in range(DEPTH):
        j = i // 2
        x = x + (0.5 * swiglu(rmsnorm(x, p['norm_ff1'][i]), p['w_ff1_up'][i], p['w_ff1_down'][i])).astype(x.dtype)
        hn = rmsnorm(x, p['norm_mix'][i])
        if i % 2 == 0:
            out, s_fin, cb = gdn_mixer(hn, gdn_s[j], gdn_conv[j], p['gdn_w_in'][j], p['gdn_conv_w'][j],
                                       p['gdn_A_log'][j], p['gdn_dt_bias'][j], p['gdn_norm_w'][j],
                                       p['gdn_w_out'][j])
            new_s.append(s_fin)
            new_conv.append(cb)
        else:
            out, hr, hi = s5_mixer(hn, s5_re[j], s5_im[j], p['s5_A_re'][j], p['s5_A_im'][j],
                                   p['s5_log_dt'][j], p['s5_B_re'][j], p['s5_B_im'][j],
                                   p['s5_C_re'][j], p['s5_C_im'][j], p['s5_D'][j], p['s5_w_glu'][j])
            new_re.append(hr)
            new_im.append(hi)
        x = x + out.astype(x.dtype)
        x = x + cross_attn(rmsnorm(x, p['norm_xa'][i]), mem_k[i], mem_v[i],
                           p['xa_w_q'][i], p['xa_w_o'][i]).astype(x.dtype)
        x = x + (0.5 * swiglu(rmsnorm(x, p['norm_ff2'][i]), p['w_ff2_up'][i], p['w_ff2_down'][i])).astype(x.dtype)
    y = rmsnorm(x, p['norm_final'])
    return y, jnp.stack(new_s), jnp.stack(new_conv), jnp.stack(new_re), jnp.stack(new_im)


def setup_inputs(seed: int = 0) -> dict:
    key = jax.random.key(seed)
    ks = iter(jax.random.split(key, 64))
    f32 = jnp.float32

    def nrm(shape, scale):
        return jax.random.normal(next(ks), shape, f32) * scale

    def gain(shape):
        return 1.0 + nrm(shape, 0.01)

    d, F = D_MODEL, D_FF
    inp = {}
    inp['x_prompt'] = nrm((BATCH, SEQ, d), 1.0)
    inp['x_sample'] = nrm((DEC_BATCH, DEC_SEQ, d), 1.0)
    inp['mem_prompt'] = nrm((BATCH, N_MEM, d), 1.0)
    inp['cache_mem_k'] = nrm((DEPTH, DEC_BATCH, N_MEM, XA_HEADS, XA_HD), 1.0)
    inp['cache_mem_v'] = nrm((DEPTH, DEC_BATCH, N_MEM, XA_HEADS, XA_HD), 1.0)
    inp['state_gdn'] = nrm((N_A, DEC_BATCH, GDN_HV, GDN_DK, GDN_DV), 0.1)
    inp['cache_gdn_conv'] = nrm((N_A, DEC_BATCH, GDN_CONV_W - 1, GDN_CONV_DIM), 1.0)
    inp['state_s5_re'] = nrm((N_B, DEC_BATCH, S5_GROUPS, S5_P), 0.5)
    inp['state_s5_im'] = nrm((N_B, DEC_BATCH, S5_GROUPS, S5_P), 0.5)
    inp['norm_ff1'] = gain((DEPTH, d))
    inp['norm_mix'] = gain((DEPTH, d))
    inp['norm_xa'] = gain((DEPTH, d))
    inp['norm_ff2'] = gain((DEPTH, d))
    inp['norm_final'] = gain((d,))
    inp['w_ff1_up'] = nrm((DEPTH, d, 2 * F), d ** -0.5)
    inp['w_ff1_down'] = nrm((DEPTH, F, d), F ** -0.5)
    inp['w_ff2_up'] = nrm((DEPTH, d, 2 * F), d ** -0.5)
    inp['w_ff2_down'] = nrm((DEPTH, F, d), F ** -0.5)
    inp['xa_w_q'] = nrm((DEPTH, d, d), d ** -0.5)
    inp['xa_w_kv'] = nrm((DEPTH, d, 2 * d), d ** -0.5)
    inp['xa_w_o'] = nrm((DEPTH, d, d), d ** -0.5)
    inp['gdn_w_in'] = nrm((N_A, d, GDN_IN_DIM), d ** -0.5)
    inp['gdn_conv_w'] = nrm((N_A, GDN_CONV_DIM, GDN_CONV_W), GDN_CONV_W ** -0.5)
    inp['gdn_A_log'] = jnp.log(jax.random.uniform(next(ks), (N_A, GDN_HV), f32, 1.0, 16.0))
    inp['gdn_dt_bias'] = nrm((N_A, GDN_HV), 0.1)
    inp['gdn_norm_w'] = gain((N_A, GDN_DV))
    inp['gdn_w_out'] = nrm((N_A, GDN_VD, d), GDN_VD ** -0.5)
    inp['s5_A_re'] = -0.5 + nrm((N_B, S5_GROUPS, S5_P), 0.01)
    inp['s5_A_im'] = jnp.pi * jnp.arange(S5_P, dtype=f32) + nrm((N_B, S5_GROUPS, S5_P), 0.01)
    inp['s5_log_dt'] = jax.random.uniform(next(ks), (N_B, S5_GROUPS), f32,
                                          math.log(1e-3), math.log(1e-1))
    inp['s5_B_re'] = nrm((N_B, S5_GROUPS, S5_P, S5_GROUP), (2 * S5_GROUP) ** -0.5)
    inp['s5_B_im'] = nrm((N_B, S5_GROUPS, S5_P, S5_GROUP), (2 * S5_GROUP) ** -0.5)
    inp['s5_C_re'] = nrm((N_B, S5_GROUPS, S5_GROUP, S5_P), S5_P ** -0.5)
    inp['s5_C_im'] = nrm((N_B, S5_GROUPS, S5_GROUP, S5_P), S5_P ** -0.5)
    inp['s5_D'] = nrm((N_B, d), 1.0)
    inp['s5_w_glu'] = nrm((N_B, d, 2 * d), d ** -0.5)
    return inp


def reference(x_prompt, x_sample, mem_prompt, cache_mem_k, cache_mem_v, state_gdn, cache_gdn_conv,
              state_s5_re, state_s5_im, norm_ff1, norm_mix, norm_xa, norm_ff2, norm_final,
              w_ff1_up, w_ff1_down, w_ff2_up, w_ff2_down, xa_w_q, xa_w_kv, xa_w_o,
              gdn_w_in, gdn_conv_w, gdn_A_log, gdn_dt_bias, gdn_norm_w, gdn_w_out,
              s5_A_re, s5_A_im, s5_log_dt, s5_B_re, s5_B_im, s5_C_re, s5_C_im, s5_D, s5_w_glu):
    f32 = jnp.float32
    p = dict(norm_ff1=norm_ff1, norm_mix=norm_mix, norm_xa=norm_xa, norm_ff2=norm_ff2,
             norm_final=norm_final, w_ff1_up=w_ff1_up, w_ff1_down=w_ff1_down,
             w_ff2_up=w_ff2_up, w_ff2_down=w_ff2_down, xa_w_q=xa_w_q, xa_w_o=xa_w_o,
             gdn_w_in=gdn_w_in, gdn_conv_w=gdn_conv_w, gdn_A_log=gdn_A_log,
             gdn_dt_bias=gdn_dt_bias, gdn_norm_w=gdn_norm_w, gdn_w_out=gdn_w_out,
             s5_A_re=s5_A_re, s5_A_im=s5_A_im, s5_log_dt=s5_log_dt, s5_B_re=s5_B_re,
             s5_B_im=s5_B_im, s5_C_re=s5_C_re, s5_C_im=s5_C_im, s5_D=s5_D, s5_w_glu=s5_w_glu)
    bp = x_prompt.shape[0]
    kv = jnp.einsum('bmd,ldf->lbmf', mem_prompt, xa_w_kv)
    kv = kv.reshape(DEPTH, bp, N_MEM, 2, XA_HEADS, XA_HD)
    mem_k_p = kv[:, :, :, 0]
    mem_v_p = kv[:, :, :, 1]
    gdn0 = jnp.zeros((N_A, bp, GDN_HV, GDN_DK, GDN_DV), f32)
    conv0 = jnp.zeros((N_A, bp, GDN_CONV_W - 1, GDN_CONV_DIM), f32)
    s5z = jnp.zeros((N_B, bp, S5_GROUPS, S5_P), f32)
    y_p, gdn_p, conv_p, s5r_p, s5i_p = trunk(x_prompt, mem_k_p, mem_v_p, gdn0, conv0, s5z, s5z, p)
    y_s, gdn_s, conv_s, s5r_s, s5i_s = trunk(x_sample, cache_mem_k, cache_mem_v, state_gdn,
                                             cache_gdn_conv, state_s5_re, state_s5_im, p)
    return (y_p, y_s, mem_k_p, mem_v_p, gdn_p, conv_p, s5r_p, s5i_p, gdn_s, conv_s, s5r_s, s5i_s)
```

```python
import functools
import math

import jax
import jax.numpy as jnp
from jax import lax
from jax.experimental import pallas as pl
from jax.experimental.pallas import tpu as pltpu

F32 = jnp.float32
BF16 = jnp.bfloat16
EPS = 1e-6

LANE = 128
SUBLANE = 8
VMEM_LIMIT = 56 * 1024 * 1024

GDN_DK = 128
GDN_DV = 128
GDN_CONV_W = 4
GDN_CHUNK = 64
S5_GROUP = 16
S5_P = 64
XA_HEADS = 4
N_MEM = 256


def _cparams(*sem):
    return pltpu.CompilerParams(dimension_semantics=sem, vmem_limit_bytes=VMEM_LIMIT)


def _silu(x):
    return x * jax.nn.sigmoid(x)


def _rms_rows(x, g):
    ms = jnp.mean(x * x, axis=-1, keepdims=True)
    return x * lax.rsqrt(ms + EPS) * g


def _dot(a, b):
    return jnp.dot(a, b, preferred_element_type=F32)


def _dot_nt(a, b):
    return lax.dot_general(a, b, (((1,), (1,)), ((), ())), preferred_element_type=F32)


def _norm_mm_body(x_ref, g_ref, *rest, n_w, epilogue, head_major):
    w_refs, o_ref, xn_ref = rest[:n_w], rest[n_w], rest[n_w + 1]

    @pl.when(pl.program_id(1) == 0)
    def _():
        xn_ref[...] = _rms_rows(x_ref[...], g_ref[...]).astype(BF16)

    xn = xn_ref[...]
    y = epilogue(*[_dot(xn, w[...].astype(BF16)) for w in w_refs])
    if head_major:
        for c in range(o_ref.shape[0]):
            o_ref[c] = y[:, c * LANE:(c + 1) * LANE].astype(o_ref.dtype)
    else:
        o_ref[...] = y.astype(o_ref.dtype)


def norm_matmul(x, gain, w, layer, col_offsets, n_out, *, tm, tn, out_dtype, epilogue, head_major=False):
    t, k = x.shape
    grid = (t // tm, n_out // tn)
    in_specs = [pl.BlockSpec((tm, k), lambda i, j: (i, 0)),
                pl.BlockSpec((1, k), lambda i, j: (0, 0))]
    for off in col_offsets:
        in_specs.append(pl.BlockSpec((None, k, tn), lambda i, j, off=off: (layer, 0, j + off)))
    if head_major:
        out_shape = jax.ShapeDtypeStruct((n_out // LANE, t, LANE), out_dtype)
        out_spec = pl.BlockSpec((tn // LANE, tm, LANE), lambda i, j: (j, i, 0))
    else:
        out_shape = jax.ShapeDtypeStruct((t, n_out), out_dtype)
        out_spec = pl.BlockSpec((tm, tn), lambda i, j: (i, j))
    body = functools.partial(_norm_mm_body, n_w=len(col_offsets), epilogue=epilogue, head_major=head_major)
    return pl.pallas_call(
        body, grid=grid, in_specs=in_specs, out_specs=out_spec, out_shape=out_shape,
        scratch_shapes=[pltpu.VMEM((tm, k), BF16)],
        compiler_params=_cparams("parallel", "arbitrary"),
    )(x, gain.reshape(1, k), *([w] * len(col_offsets)))


def _mm_res_body(a_ref, *rest, n_w, epilogue, a_head_major):
    w_refs, r_ref, o_ref = rest[:n_w], rest[n_w], rest[n_w + 1]
    if a_head_major:
        a = jnp.concatenate([a_ref[h] for h in range(a_ref.shape[0])], axis=-1)
    else:
        a = a_ref[...]
    o_ref[...] = r_ref[...] + epilogue(*[_dot(a, w[...].astype(BF16)) for w in w_refs])


def matmul_residual(a, w, layer, col_offsets, res, *, tm, tn, epilogue, a_head_major=False):
    t, n_out = res.shape
    k = w.shape[1]
    grid = (t // tm, n_out // tn)
    if a_head_major:
        a_spec = pl.BlockSpec((k // LANE, tm, LANE), lambda i, j: (0, i, 0))
    else:
        a_spec = pl.BlockSpec((tm, k), lambda i, j: (i, 0))
    in_specs = [a_spec]
    for off in col_offsets:
        in_specs.append(pl.BlockSpec((None, k, tn), lambda i, j, off=off: (layer, 0, j + off)))
    in_specs.append(pl.BlockSpec((tm, tn), lambda i, j: (i, j)))
    body = functools.partial(_mm_res_body, n_w=len(col_offsets), epilogue=epilogue, a_head_major=a_head_major)
    return pl.pallas_call(
        body, grid=grid, in_specs=in_specs,
        out_specs=pl.BlockSpec((tm, tn), lambda i, j: (i, j)),
        out_shape=jax.ShapeDtypeStruct((t, n_out), F32),
        compiler_params=_cparams("parallel", "arbitrary"),
    )(a, *([w] * len(col_offsets)), res)


def _norm_body(x_ref, g_ref, o_ref):
    o_ref[...] = _rms_rows(x_ref[...], g_ref[...]).astype(o_ref.dtype)


def rmsnorm_rows(x, gain, *, tm):
    t, k = x.shape
    return pl.pallas_call(
        _norm_body, grid=(t // tm,),
        in_specs=[pl.BlockSpec((tm, k), lambda i: (i, 0)), pl.BlockSpec((1, k), lambda i: (0, 0))],
        out_specs=pl.BlockSpec((tm, k), lambda i: (i, 0)),
        out_shape=jax.ShapeDtypeStruct((t, k), F32),
        compiler_params=_cparams("parallel"),
    )(x, gain.reshape(1, k))


def _kv_body(m_ref, wk_ref, wv_ref, k_ref, v_ref, m16_ref):
    @pl.when((pl.program_id(0) == 0) & (pl.program_id(1) == 0))
    def _():
        m16_ref[...] = m_ref[...].astype(BF16)

    m16 = m16_ref[...]
    k_ref[...] = _dot(m16, wk_ref[...].astype(BF16))
    v_ref[...] = _dot(m16, wv_ref[...].astype(BF16))


def memory_kv(mem, w_kv, *, tn):
    r, d = mem.shape
    nl = w_kv.shape[0]
    nj = d // tn
    return pl.pallas_call(
        _kv_body, grid=(nl, nj),
        in_specs=[pl.BlockSpec((r, d), lambda l, j: (0, 0)),
                  pl.BlockSpec((None, d, tn), lambda l, j: (l, 0, j)),
                  pl.BlockSpec((None, d, tn), lambda l, j: (l, 0, j + nj))],
        out_specs=[pl.BlockSpec((None, r, tn), lambda l, j: (l, 0, j)),
                   pl.BlockSpec((None, r, tn), lambda l, j: (l, 0, j))],
        out_shape=[jax.ShapeDtypeStruct((nl, r, d), F32)] * 2,
        scratch_shapes=[pltpu.VMEM((r, d), BF16)],
        compiler_params=_cparams("arbitrary", "arbitrary"),
    )(mem, w_kv, w_kv)


def _xattn_body(q_ref, k_ref, v_ref, o_ref, *, heads):
    nb, _, d = q_ref.shape
    hd = d // heads
    scale = hd ** -0.5
    for b in range(nb):
        for h in range(heads):
            sl = slice(h * hd, (h + 1) * hd)
            q = q_ref[b, :, sl]
            k = k_ref[b, :, sl].astype(BF16)
            v = v_ref[b, :, sl].astype(BF16)
            s = _dot_nt(q, k) * scale
            p = jnp.exp(s - jnp.max(s, axis=-1, keepdims=True))
            den = jnp.sum(p, axis=-1, keepdims=True)
            o_ref[b, :, sl] = (_dot(p.astype(BF16), v) / den).astype(o_ref.dtype)


def cross_attention(q, k, v, *, nb, tl):
    bsz, l, d = q.shape
    m = k.shape[1]
    return pl.pallas_call(
        functools.partial(_xattn_body, heads=XA_HEADS), grid=(bsz // nb, l // tl),
        in_specs=[pl.BlockSpec((nb, tl, d), lambda b, i: (b, i, 0)),
                  pl.BlockSpec((nb, m, d), lambda b, i: (b, 0, 0)),
                  pl.BlockSpec((nb, m, d), lambda b, i: (b, 0, 0))],
        out_specs=pl.BlockSpec((nb, tl, d), lambda b, i: (b, i, 0)),
        out_shape=jax.ShapeDtypeStruct((bsz, l, d), BF16),
        compiler_params=_cparams("parallel", "arbitrary"),
    )(q, k, v)


def _gelu_tanh(y):
    return 0.5 * y * (1.0 + jnp.tanh(0.7978845608028654 * (y + 0.044715 * (y * y * y))))


def _swap_halves(x):
    r, c = x.shape
    return pltpu.roll(x.reshape(r // SUBLANE, SUBLANE, c), 4, 1).reshape(r, c)


def _s5_body(u_ref, b_ref, lam_ref, h0_ref, c_ref, d_ref, y_ref, hfin_ref, hs_ref, carry_ref):
    tb, r, _ = u_ref.shape
    rows = tb * r
    nstate = hs_ref.shape[1]
    c_idx = pl.program_id(1)

    @pl.when(c_idx == 0)
    def _():
        carry_ref[...] = h0_ref[...]

    u = u_ref[...].reshape(rows, LANE)
    is_re = (lax.broadcasted_iota(jnp.int32, (rows, 1), 0) % SUBLANE) < 4
    u16 = u.astype(BF16)
    z16 = jnp.zeros_like(u16)
    lhs = jnp.concatenate([jnp.where(is_re, u16, z16), jnp.where(is_re, z16, u16)], axis=1)
    hs_ref[...] = _dot(lhs, b_ref[0].astype(BF16))

    a_pl = jnp.tile(lam_ref[0, 0], (r // SUBLANE, 1))
    b_pl = jnp.tile(lam_ref[0, 1], (r // SUBLANE, 1))

    def step(t, h):
        r0 = pl.multiple_of(t * r, r)
        h = a_pl * h + b_pl * _swap_halves(h) + hs_ref[pl.ds(r0, r), :]
        hs_ref[pl.ds(r0, r), :] = h
        return h

    h = lax.fori_loop(0, tb, step, carry_ref[...], unroll=min(tb, 8))
    carry_ref[...] = h

    y2 = _dot(hs_ref[...].astype(BF16), c_ref[0].astype(BF16))
    ysel = jnp.where(is_re, y2[:, :LANE], y2[:, LANE:])
    y = ysel + _swap_halves(ysel) + d_ref[...] * u
    y_ref[...] = _gelu_tanh(y).reshape(tb, r, LANE)

    @pl.when(c_idx == pl.num_programs(1) - 1)
    def _():
        hfin_ref[...] = h


def s5_core(u, bcat, lam, h0, ccat, dskip, *, tb):
    l, r, d = u.shape
    nblk = d // LANE
    nstate = bcat.shape[2]
    return pl.pallas_call(
        _s5_body, grid=(nblk, l // tb),
        in_specs=[pl.BlockSpec((tb, r, LANE), lambda j, c: (c, 0, j)),
                  pl.BlockSpec((1, 2 * LANE, nstate), lambda j, c: (j, 0, 0)),
                  pl.BlockSpec((1, 2, SUBLANE, nstate), lambda j, c: (j, 0, 0, 0)),
                  pl.BlockSpec((r, nstate), lambda j, c: (0, j)),
                  pl.BlockSpec((1, nstate, 2 * LANE), lambda j, c: (j, 0, 0)),
                  pl.BlockSpec((1, LANE), lambda j, c: (0, j))],
        out_specs=[pl.BlockSpec((tb, r, LANE), lambda j, c: (c, 0, j)),
                   pl.BlockSpec((r, nstate), lambda j, c: (0, j))],
        out_shape=[jax.ShapeDtypeStruct((l, r, d), F32),
                   jax.ShapeDtypeStruct((r, nblk * nstate), F32)],
        scratch_shapes=[pltpu.VMEM((tb * r, nstate), F32), pltpu.VMEM((r, nstate), F32)],
        compiler_params=_cparams("parallel", "arbitrary"),
    )(u, bcat, lam, h0, ccat, dskip.reshape(1, d))


def s5_params(a_re, a_im, log_dt, b_re, b_im, c_re, c_im):
    g, p = a_re.shape
    gc = b_re.shape[2]
    gpb = LANE // gc
    nblk = g // gpb
    dt = jnp.exp(log_dt)[:, None]
    mag = jnp.exp(a_re * dt)
    lb_re = mag * jnp.cos(a_im * dt)
    lb_im = mag * jnp.sin(a_im * dt)
    den = a_re * a_re + a_im * a_im
    nr = lb_re - 1.0
    f_re = (nr * a_re + lb_im * a_im) / den
    f_im = (lb_im * a_re - nr * a_im) / den
    bb_re = f_re[..., None] * b_re - f_im[..., None] * b_im
    bb_im = f_re[..., None] * b_im + f_im[..., None] * b_re
    eye = jnp.eye(gpb, dtype=F32)

    def blockdiag_in(bb):
        x = bb.reshape(nblk, gpb, p, gc)
        return jnp.einsum('jgpc,gh->jgchp', x, eye).reshape(nblk, gpb * gc, gpb * p)

    def blockdiag_out(cc):
        x = cc.reshape(nblk, gpb, gc, p)
        return jnp.einsum('jgcp,gh->jgphc', x, eye).reshape(nblk, gpb * p, gpb * gc)

    bcat = jnp.concatenate([blockdiag_in(bb_re), blockdiag_in(bb_im)], axis=1)
    ccat = jnp.concatenate([blockdiag_out(c_re), -blockdiag_out(c_im)], axis=2)
    lre = lb_re.reshape(nblk, 1, gpb * p)
    lim = lb_im.reshape(nblk, 1, gpb * p)
    a_pl = jnp.broadcast_to(lre, (nblk, SUBLANE, gpb * p))
    b_pl = jnp.concatenate([jnp.broadcast_to(-lim, (nblk, 4, gpb * p)),
                            jnp.broadcast_to(lim, (nblk, 4, gpb * p))], axis=1)
    lam = jnp.stack([a_pl, b_pl], axis=1)
    return bcat, lam, ccat


def _s5_pack_rows(x):
    l, b, n = x.shape
    x = x.reshape(l, b // 4, 4, n)
    return jnp.concatenate([x, x], axis=2).reshape(l, 2 * b, n)


def _s5_pack_state(h_re, h_im):
    b = h_re.shape[0]
    n = h_re.shape[1] * h_re.shape[2]
    x = jnp.concatenate([h_re.reshape(b // 4, 4, n), h_im.reshape(b // 4, 4, n)], axis=1)
    return x.reshape(2 * b, n)


def _s5_unpack_state(h, g, p):
    b2, n = h.shape
    x = h.reshape(b2 // SUBLANE, SUBLANE, n)
    return (x[:, :4].reshape(b2 // 2, g, p), x[:, 4:].reshape(b2 // 2, g, p))


def _softplus(x):
    return jnp.maximum(x, 0.0) + jnp.log(1.0 + jnp.exp(-jnp.abs(x)))


def _lane_pick(x, idx):
    lane = lax.broadcasted_iota(jnp.int32, x.shape, 1)
    return jnp.sum(jnp.where(lane == idx, x, 0.0), axis=1, keepdims=True)


def _gdn_prompt_body(p_ref, ba_ref, cw_ref, alog_ref, dtb_ref, nw_ref, s0_ref, c0_ref,
                     og_ref, sout_ref, cout_ref, s_ref, tail_ref, xp_ref):
    c = GDN_CHUNK
    n_idx = pl.program_id(1)
    nkh = s_ref.shape[0] // 2
    nconv = tail_ref.shape[0]

    @pl.when(n_idx == 0)
    def _():
        s_ref[...] = s0_ref[0]
        tail_ref[...] = c0_ref[0]

    ba = ba_ref[...]
    beta_all = jax.nn.sigmoid(ba[:, :LANE])
    g_all = -jnp.exp(alog_ref[...]) * _softplus(ba[:, LANE:] + dtb_ref[...])
    ri = lax.broadcasted_iota(jnp.int32, (c, c), 0)
    ci = lax.broadcasted_iota(jnp.int32, (c, c), 1)
    gc_all = jnp.dot((ri >= ci).astype(F32), g_all, preferred_element_type=F32,
                     precision=lax.Precision.HIGHEST)

    r2 = lax.broadcasted_iota(jnp.int32, (2 * c, 2 * c), 0)
    c2 = lax.broadcasted_iota(jnp.int32, (2 * c, 2 * c), 1)
    same = (r2 // c) == (c2 // c)
    incl = same & (r2 >= c2)
    strict = same & (r2 > c2)
    eye2 = (r2 == c2).astype(F32)
    top = lax.broadcasted_iota(jnp.int32, (2 * c, 1), 0) < c

    def conv(slot, cb):
        x = p_ref[cb]
        xp_ref[slot, 0:SUBLANE] = tail_ref[cb]
        xp_ref[slot, SUBLANE:SUBLANE + c] = x
        acc = xp_ref[slot, 5:5 + c] * cw_ref[cb]
        for j in range(1, GDN_CONV_W):
            acc = acc + xp_ref[slot, 5 + j:5 + j + c] * cw_ref[j * nconv + cb]
        tail_ref[cb] = x[c - SUBLANE:c]
        return _silu(acc)

    def pair(hk, carry):
        q = conv(0, hk)
        k = conv(1, nkh + hk)
        v0 = conv(2, 2 * nkh + 2 * hk)
        v1 = conv(3, 2 * nkh + 2 * hk + 1)
        qn = q * lax.rsqrt(jnp.sum(q * q, axis=-1, keepdims=True) + EPS) * (GDN_DK ** -0.5)
        kn = k * lax.rsqrt(jnp.sum(k * k, axis=-1, keepdims=True) + EPS)
        h0 = 2 * hk
        beta = jnp.concatenate([_lane_pick(beta_all, h0), _lane_pick(beta_all, h0 + 1)], axis=0)
        gc = jnp.concatenate([_lane_pick(gc_all, h0), _lane_pick(gc_all, h0 + 1)], axis=0)
        gl0 = gc[c - 1:c]
        gl1 = gc[2 * c - 1:2 * c]
        gl = jnp.where(top, gl0, gl1)
        kn2 = jnp.concatenate([kn, kn], axis=0)
        qn2 = jnp.concatenate([qn, qn], axis=0)
        v2 = jnp.concatenate([v0, v1], axis=0)
        kn16 = kn2.astype(BF16)
        gcb = jnp.broadcast_to(gc, (2 * c, 2 * c))
        decay = jnp.exp(jnp.where(incl, gcb - gcb.T, -1e30))
        kk = _dot_nt(kn16, kn16)
        qk = _dot_nt(qn2.astype(BF16), kn16)
        attn = qk * decay
        nmat = jnp.where(strict, -(beta * kk * decay), 0.0)
        tinv = eye2 + nmat
        m16 = nmat.astype(BF16)
        for _ in range(int(math.log2(c)) - 1):
            mm = _dot(m16, m16)
            m16 = mm.astype(BF16)
            tinv = tinv + _dot(tinv.astype(BF16), m16)
        eg = jnp.exp(gc)
        rhs = jnp.concatenate([v2 * beta, kn2 * (beta * eg)], axis=1).astype(BF16)
        uw = _dot(tinv.astype(BF16), rhs)
        u, w = uw[:, :GDN_DV], uw[:, GDN_DV:]
        qd = qn2 * eg
        s_a = s_ref[h0]
        s_b = s_ref[h0 + 1]
        res_a = _dot(jnp.concatenate([w[:c], qd[:c]], axis=0).astype(BF16), s_a.astype(BF16))
        res_b = _dot(jnp.concatenate([w[c:], qd[c:]], axis=0).astype(BF16), s_b.astype(BF16))
        v_new = u - jnp.concatenate([res_a[:c], res_b[:c]], axis=0)
        v_new16 = v_new.astype(BF16)
        o = jnp.concatenate([res_a[c:], res_b[c:]], axis=0) + _dot(attn.astype(BF16), v_new16)
        kd = kn2 * jnp.exp(gl - gc)
        kd_a = jnp.where(top, kd, 0.0).T.astype(BF16)
        kd_b = jnp.where(top, 0.0, kd).T.astype(BF16)
        s_ref[h0] = s_a * jnp.exp(gl0) + _dot(kd_a, v_new16)
        s_ref[h0 + 1] = s_b * jnp.exp(gl1) + _dot(kd_b, v_new16)
        z = jnp.concatenate([p_ref[4 * nkh + h0], p_ref[4 * nkh + h0 + 1]], axis=0)
        og = o * lax.rsqrt(jnp.mean(o * o, axis=-1, keepdims=True) + EPS) * nw_ref[...] * _silu(z)
        og_ref[h0] = og[:c].astype(og_ref.dtype)
        og_ref[h0 + 1] = og[c:].astype(og_ref.dtype)
        return carry

    lax.fori_loop(0, nkh, pair, 0)

    @pl.when(n_idx == pl.num_programs(1) - 1)
    def _():
        sout_ref[0] = s_ref[...]
        cout_ref[0] = tail_ref[...]


def gdn_prompt_core(proj, ba, cw, alog, dtb, nw, s0, conv0, *, bsz, seq, row0):
    nblk = proj.shape[0]
    nvh = s0.shape[1]
    nconv = conv0.shape[1]
    c = GDN_CHUNK
    nc = seq // c
    blk0 = row0 // c
    return pl.pallas_call(
        _gdn_prompt_body, grid=(bsz, nc),
        in_specs=[pl.BlockSpec((nblk, c, LANE), lambda b, n: (0, blk0 + b * nc + n, 0)),
                  pl.BlockSpec((c, 2 * LANE), lambda b, n: (blk0 + b * nc + n, 0)),
                  pl.BlockSpec((GDN_CONV_W * nconv, 1, LANE), lambda b, n: (0, 0, 0)),
                  pl.BlockSpec((1, LANE), lambda b, n: (0, 0)),
                  pl.BlockSpec((1, LANE), lambda b, n: (0, 0)),
                  pl.BlockSpec((1, LANE), lambda b, n: (0, 0)),
                  pl.BlockSpec((1, nvh, GDN_DK, GDN_DV), lambda b, n: (b, 0, 0, 0)),
                  pl.BlockSpec((1, nconv, SUBLANE, LANE), lambda b, n: (b, 0, 0, 0))],
        out_specs=[pl.BlockSpec((nvh, c, LANE), lambda b, n: (0, b * nc + n, 0)),
                   pl.BlockSpec((1, nvh, GDN_DK, GDN_DV), lambda b, n: (b, 0, 0, 0)),
                   pl.BlockSpec((1, nconv, SUBLANE, LANE), lambda b, n: (b, 0, 0, 0))],
        out_shape=[jax.ShapeDtypeStruct((nvh, bsz * seq, LANE), BF16),
                   jax.ShapeDtypeStruct((bsz, nvh, GDN_DK, GDN_DV), F32),
                   jax.ShapeDtypeStruct((bsz, nconv, SUBLANE, LANE), F32)],
        scratch_shapes=[pltpu.VMEM((nvh, GDN_DK, GDN_DV), F32),
                        pltpu.VMEM((nconv, SUBLANE, LANE), F32),
                        pltpu.VMEM((4, SUBLANE + c, LANE), F32)],
        compiler_params=_cparams("parallel", "arbitrary"),
    )(proj, ba, cw, alog, dtb, nw, s0, conv0)


SAMPLE_BB = 8
SAMPLE_KH = 4


def _gdn_sample_body(*refs, seq):
    nt = seq
    q_refs = refs[0:nt]
    k_refs = refs[nt:2 * nt]
    v_refs = refs[2 * nt:3 * nt]
    z_refs = refs[3 * nt:4 * nt]
    ba_refs = refs[4 * nt:5 * nt]
    (cwq_ref, cwk_ref, cwv_ref, alog_ref, dtb_ref, nw_ref, s0_ref, c0q_ref, c0k_ref, c0v_ref,
     og_ref, sout_ref, cq_ref, ck_ref, cv_ref, xt_ref) = refs[5 * nt:]
    nkh = SAMPLE_KH
    bb = SAMPLE_BB
    hg = pl.program_id(1)
    nhist = GDN_CONV_W - 1

    betas, decays = [], []
    for t in range(nt):
        ba = ba_refs[t][...]
        betas.append(jax.nn.sigmoid(ba[:, :LANE]))
        decays.append(jnp.exp(-jnp.exp(alog_ref[...]) * _softplus(ba[:, LANE:] + dtb_ref[...])))

    def conv(x_refs, c0_ref, cw_ref, cout_ref, cb):
        xp = [c0_ref[j, cb] for j in range(nhist)] + [r[cb] for r in x_refs]
        outs = []
        for t in range(nt):
            acc = xp[t] * cw_ref[0, cb]
            for j in range(1, GDN_CONV_W):
                acc = acc + xp[t + j] * cw_ref[j, cb]
            outs.append(_silu(acc))
        for j in range(nhist):
            cout_ref[j, cb] = xp[nt + j]
        return outs

    def khead(hkl, carry):
        qs = conv(q_refs, c0q_ref, cwq_ref, cq_ref, hkl)
        ks = conv(k_refs, c0k_ref, cwk_ref, ck_ref, hkl)
        qs = [q * lax.rsqrt(jnp.sum(q * q, axis=-1, keepdims=True) + EPS) * (GDN_DK ** -0.5) for q in qs]
        ks = [k * lax.rsqrt(jnp.sum(k * k, axis=-1, keepdims=True) + EPS) for k in ks]
        pad = jnp.zeros((LANE - 2 * nt * bb, LANE), F32)
        xt_ref[...] = jnp.concatenate(ks + qs + [pad], axis=0).T
        for e in range(2):
            hl = 2 * hkl + e
            h_glob = (hg * nkh + hkl) * 2 + e
            vs = conv(v_refs, c0v_ref, cwv_ref, cv_ref, hl)
            beta_c = [_lane_pick(betas[t], h_glob) for t in range(nt)]
            dec_c = [_lane_pick(decays[t], h_glob) for t in range(nt)]
            o_rows = [[None] * bb for _ in range(nt)]
            for b in range(bb):
                s = s0_ref[b, hl]
                for t in range(nt):
                    kc = xt_ref[:, t * bb + b:t * bb + b + 1]
                    qc = xt_ref[:, (nt + t) * bb + b:(nt + t) * bb + b + 1]
                    sd = s * dec_c[t][b:b + 1]
                    pred = jnp.sum(sd * kc, axis=0, keepdims=True)
                    delta = beta_c[t][b:b + 1] * (vs[t][b:b + 1] - pred)
                    s = sd + kc * delta
                    o_rows[t][b] = jnp.sum(s * qc, axis=0, keepdims=True)
                sout_ref[b, hl] = s
            for t in range(nt):
                o = jnp.concatenate(o_rows[t], axis=0)
                z = z_refs[t][hl]
                og = o * lax.rsqrt(jnp.mean(o * o, axis=-1, keepdims=True) + EPS) * nw_ref[...] * _silu(z)
                og_ref[t, hl] = og
        return carry

    lax.fori_loop(0, nkh, khead, 0)


def gdn_sample_core(proj, ba, cwq, cwk, cwv, alog, dtb, nw, s0, c0q, c0k, c0v, *, bsz, seq, row0):
    nvh = s0.shape[1]
    nkh_all = nvh // 2
    bb, kh = SAMPLE_BB, SAMPLE_KH
    nhist = GDN_CONV_W - 1
    rb0 = row0 // bb
    nbb = bsz // bb
    grid = (nbb, nkh_all // kh)

    def tok(off, width):
        return [pl.BlockSpec((width, bb, LANE), lambda i, g, t=t: (off // width + g, rb0 + t * nbb + i, 0))
                for t in range(seq)]

    in_specs = (tok(0, kh) + tok(nkh_all, kh) + tok(2 * nkh_all, 2 * kh) + tok(4 * nkh_all, 2 * kh)
                + [pl.BlockSpec((bb, 2 * LANE), lambda i, g, t=t: (rb0 + t * nbb + i, 0)) for t in range(seq)]
                + [pl.BlockSpec((GDN_CONV_W, kh, 1, LANE), lambda i, g: (0, g, 0, 0)),
                   pl.BlockSpec((GDN_CONV_W, kh, 1, LANE), lambda i, g: (0, g, 0, 0)),
                   pl.BlockSpec((GDN_CONV_W, 2 * kh, 1, LANE), lambda i, g: (0, g, 0, 0)),
                   pl.BlockSpec((1, LANE), lambda i, g: (0, 0)),
                   pl.BlockSpec((1, LANE), lambda i, g: (0, 0)),
                   pl.BlockSpec((1, LANE), lambda i, g: (0, 0)),
                   pl.BlockSpec((bb, 2 * kh, GDN_DK, GDN_DV), lambda i, g: (i, g, 0, 0)),
                   pl.BlockSpec((nhist, kh, bb, LANE), lambda i, g: (0, g, i, 0)),
                   pl.BlockSpec((nhist, kh, bb, LANE), lambda i, g: (0, g, i, 0)),
                   pl.BlockSpec((nhist, 2 * kh, bb, LANE), lambda i, g: (0, g, i, 0))])
    out_specs = [pl.BlockSpec((seq, 2 * kh, bb, LANE), lambda i, g: (0, g, i, 0)),
                 pl.BlockSpec((bb, 2 * kh, GDN_DK, GDN_DV), lambda i, g: (i, g, 0, 0)),
                 pl.BlockSpec((nhist, kh, bb, LANE), lambda i, g: (0, g, i, 0)),
                 pl.BlockSpec((nhist, kh, bb, LANE), lambda i, g: (0, g, i, 0)),
                 pl.BlockSpec((nhist, 2 * kh, bb, LANE), lambda i, g: (0, g, i, 0))]
    out_shape = [jax.ShapeDtypeStruct((seq, nvh, bsz, LANE), F32),
                 jax.ShapeDtypeStruct(s0.shape, F32),
                 jax.ShapeDtypeStruct(c0q.shape, F32),
                 jax.ShapeDtypeStruct(c0k.shape, F32),
                 jax.ShapeDtypeStruct(c0v.shape, F32)]
    return pl.pallas_call(
        functools.partial(_gdn_sample_body, seq=seq), grid=grid,
        in_specs=in_specs, out_specs=out_specs, out_shape=out_shape,
        scratch_shapes=[pltpu.VMEM((LANE, LANE), F32)],
        compiler_params=_cparams("parallel", "arbitrary"),
    )(*([proj] * (4 * seq)), *([ba] * seq), cwq, cwk, cwv, alog, dtb, nw, s0, c0q, c0k, c0v)


TM = 1088


def _pad_lanes(v, n=LANE):
    return jnp.pad(v, (0, n - v.shape[0])).reshape(1, n)


def kernel(x_prompt, x_sample, mem_prompt, cache_mem_k, cache_mem_v, state_gdn, cache_gdn_conv, state_s5_re, state_s5_im, norm_ff1, norm_mix, norm_xa, norm_ff2, norm_final, w_ff1_up, w_ff1_down, w_ff2_up, w_ff2_down, xa_w_q, xa_w_kv, xa_w_o, gdn_w_in, gdn_conv_w, gdn_A_log, gdn_dt_bias, gdn_norm_w, gdn_w_out, s5_A_re, s5_A_im, s5_log_dt, s5_B_re, s5_B_im, s5_C_re, s5_C_im, s5_D, s5_w_glu):
    bp, lp, d = x_prompt.shape
    bs, ls, _ = x_sample.shape
    depth = norm_ff1.shape[0]
    d_ff = w_ff1_down.shape[1]
    tp, ts = bp * lp, bs * ls
    nvh = gdn_A_log.shape[1]
    nkh = nvh // 2
    kd = nkh * GDN_DK
    vd = nvh * GDN_DV
    conv_dim = 2 * kd + vd
    nconv = conv_dim // LANE
    nhist = GDN_CONV_W - 1
    g5, p5 = s5_A_re.shape[1], s5_A_re.shape[2]

    x = jnp.concatenate([x_prompt.reshape(tp, d), x_sample.transpose(1, 0, 2).reshape(ts, d)], axis=0)

    mem_k_p, mem_v_p = memory_kv(mem_prompt.reshape(bp * N_MEM, d), xa_w_kv, tn=512)
    mem_k_p = mem_k_p.reshape(depth, bp, N_MEM, d)
    mem_v_p = mem_v_p.reshape(depth, bp, N_MEM, d)

    half = lambda acc: 0.5 * acc
    ident = lambda acc: acc
    swiglu = lambda a, b: _silu(a) * b
    glu = lambda a, b: a * jax.nn.sigmoid(b)

    def ffn(x, gain, w_up, w_down, i):
        h = norm_matmul(x, gain[i], w_up, i, (0, d_ff // 512), d_ff, tm=TM, tn=512,
                        out_dtype=BF16, epilogue=swiglu)
        return matmul_residual(h, w_down, i, (0,), x, tm=TM, tn=256, epilogue=half)

    gdn_p, conv_p, gdn_s, conv_s = [], [], [], []
    s5r_p, s5i_p, s5r_s, s5i_s = [], [], [], []
    for i in range(depth):
        j = i // 2
        x = ffn(x, norm_ff1, w_ff1_up, w_ff1_down, i)

        if i % 2 == 0:
            proj = norm_matmul(x, norm_mix[i], gdn_w_in, j, (0,), conv_dim + vd, tm=TM, tn=512,
                               out_dtype=F32, epilogue=ident, head_major=True)
            w_ba = jnp.zeros((1, d, 2 * LANE), F32)
            w_ba = w_ba.at[0, :, :nvh].set(gdn_w_in[j][:, conv_dim + vd:conv_dim + vd + nvh])
            w_ba = w_ba.at[0, :, LANE:LANE + nvh].set(gdn_w_in[j][:, conv_dim + vd + nvh:])
            ba = norm_matmul(x, norm_mix[i], w_ba, 0, (0,), 2 * LANE, tm=TM, tn=2 * LANE,
                             out_dtype=F32, epilogue=ident)
            cw = gdn_conv_w[j].T.reshape(GDN_CONV_W, nconv, 1, LANE)
            alog = _pad_lanes(gdn_A_log[j])
            dtb = _pad_lanes(gdn_dt_bias[j])
            nw = gdn_norm_w[j].reshape(1, GDN_DV)
            og_p, s_p, tail_p = gdn_prompt_core(
                proj, ba, cw.reshape(GDN_CONV_W * nconv, 1, LANE), alog, dtb, nw,
                jnp.zeros((bp, nvh, GDN_DK, GDN_DV), F32), jnp.zeros((bp, nconv, SUBLANE, LANE), F32),
                bsz=bp, seq=lp, row0=0)
            gdn_p.append(s_p)
            conv_p.append(tail_p[:, :, SUBLANE - nhist:, :].transpose(0, 2, 1, 3).reshape(bp, nhist, conv_dim))
            c0 = cache_gdn_conv[j].transpose(1, 0, 2).reshape(nhist, bs, nconv, LANE).transpose(0, 2, 1, 3)
            og_s, s_s, cq, ck, cv = gdn_sample_core(
                proj, ba, cw[:, :nkh], cw[:, nkh:2 * nkh], cw[:, 2 * nkh:], alog, dtb, nw,
                state_gdn[j], c0[:, :nkh], c0[:, nkh:2 * nkh], c0[:, 2 * nkh:],
                bsz=bs, seq=ls, row0=tp)
            gdn_s.append(s_s)
            cnew = jnp.concatenate([cq, ck, cv], axis=1)
            conv_s.append(cnew.transpose(2, 0, 1, 3).reshape(bs, nhist, conv_dim))
            og = jnp.concatenate([og_p, og_s.transpose(1, 0, 2, 3).reshape(nvh, ts, LANE).astype(BF16)], axis=1)
            x = matmul_residual(og, gdn_w_out, j, (0,), x, tm=TM, tn=256, epilogue=ident, a_head_major=True)
        else:
            hn = rmsnorm_rows(x, norm_mix[i], tm=TM)
            bcat, lam, ccat = s5_params(s5_A_re[j], s5_A_im[j], s5_log_dt[j], s5_B_re[j], s5_B_im[j],
                                        s5_C_re[j], s5_C_im[j])
            u_p = _s5_pack_rows(hn[:tp].reshape(bp, lp, d).transpose(1, 0, 2))
            y_p, hf_p = s5_core(u_p, bcat, lam, jnp.zeros((2 * bp, g5 * p5), F32), ccat, s5_D[j], tb=256)
            u_s = _s5_pack_rows(hn[tp:].reshape(ls, bs, d))
            y_s, hf_s = s5_core(u_s, bcat, lam, _s5_pack_state(state_s5_re[j], state_s5_im[j]), ccat,
                                s5_D[j], tb=ls)
            re_p, im_p = _s5_unpack_state(hf_p, g5, p5)
            re_s, im_s = _s5_unpack_state(hf_s, g5, p5)
            s5r_p.append(re_p); s5i_p.append(im_p); s5r_s.append(re_s); s5i_s.append(im_s)
            y_p = y_p.reshape(lp, bp // 4, SUBLANE, d)[:, :, :4].reshape(lp, bp, d).transpose(1, 0, 2)
            y_s = y_s.reshape(ls, bs // 4, SUBLANE, d)[:, :, :4].reshape(ts, d)
            yg = jnp.concatenate([y_p.reshape(tp, d), y_s], axis=0).astype(BF16)
            x = matmul_residual(yg, s5_w_glu, j, (0, d // 512), x, tm=TM, tn=512, epilogue=glu)

        q = norm_matmul(x, norm_xa[i], xa_w_q, i, (0,), d, tm=TM, tn=512, out_dtype=BF16, epilogue=ident)
        o_p = cross_attention(q[:tp].reshape(bp, lp, d), mem_k_p[i], mem_v_p[i], nb=1, tl=512)
        lpad = 16
        q_s = jnp.pad(q[tp:].reshape(ls, bs, d).transpose(1, 0, 2), ((0, 0), (0, lpad - ls), (0, 0)))
        o_s = cross_attention(q_s, cache_mem_k[i].reshape(bs, N_MEM, d), cache_mem_v[i].reshape(bs, N_MEM, d),
                              nb=2, tl=lpad)
        o = jnp.concatenate([o_p.reshape(tp, d), o_s[:, :ls].transpose(1, 0, 2).reshape(ts, d)], axis=0)
        x = matmul_residual(o, xa_w_o, i, (0,), x, tm=TM, tn=512, epilogue=ident)

        x = ffn(x, norm_ff2, w_ff2_up, w_ff2_down, i)

    y = rmsnorm_rows(x, norm_final, tm=TM)
    y_p = y[:tp].reshape(bp, lp, d)
    y_s = y[tp:].reshape(ls, bs, d).transpose(1, 0, 2)
    mk = mem_k_p.reshape(depth, bp, N_MEM, XA_HEADS, d // XA_HEADS)
    mv = mem_v_p.reshape(depth, bp, N_MEM, XA_HEADS, d // XA_HEADS)
    return (y_p, y_s, mk, mv, jnp.stack(gdn_p), jnp.stack(conv_p), jnp.stack(s5r_p), jnp.stack(s5i_p),
            jnp.stack(gdn_s), jnp.stack(conv_s), jnp.stack(s5r_s), jnp.stack(s5i_s))
```
